```python
import math
import jax, jax.numpy as jnp
from jax import lax
import numpy as np

D_MODEL = 1024
BATCH = 16
SEQ = 4096
DEPTH = 1
DEC_BATCH = 16
DEC_SEQ = 64
PAST_LEN = 4096

CHUNK = 64
Q_BLOCK = 128
N_META = 16
EPS = 1e-6
A_HEADS = 4
A_DH = 64
A_DV = 2 * A_DH
A_ROT = A_DH // 4
A_THETA = 500000.0
R_HEADS = 4
R_DK = 64
R_DV = 128
R_THETA = 10000.0
A_Q = A_HEADS * 2 * A_DH
A_K = A_HEADS * 2 * A_DH
A_V = A_HEADS * A_DV
R_Q = R_HEADS * R_DK
R_K = R_HEADS * R_DK
R_V = R_HEADS * R_DV
R_G = R_HEADS * R_DV
P_IN = A_Q + A_K + A_V + R_Q + R_K + R_V + R_G
SPLITS = (A_Q, A_Q + A_K, A_Q + A_K + A_V, A_Q + A_K + A_V + R_Q,
          A_Q + A_K + A_V + R_Q + R_K, A_Q + A_K + A_V + R_Q + R_K + R_V)
MIX = A_HEADS * A_DV + R_HEADS * R_DV
D_FF = ((-(-8 * D_MODEL // 3) + 255) // 256) * 256

kernel_name = "hymba_diffattn_retention_stream_step"


def rms_norm(x, g=None):
    xf = x.astype(jnp.float32)
    y = xf * lax.rsqrt(jnp.mean(xf * xf, axis=-1, keepdims=True) + EPS)
    if g is not None:
        y = y * g.astype(jnp.float32)
    return y.astype(x.dtype)


def rope(x, pos, rot_dim, theta):
    half = rot_dim // 2
    inv = theta ** (-jnp.arange(half, dtype=jnp.float32) / half)
    ang = pos.astype(jnp.float32)[:, None] * inv[None, :]
    ang = ang.reshape((pos.shape[0],) + (1,) * (x.ndim - 3) + (half,))
    cos, sin = jnp.cos(ang), jnp.sin(ang)
    xr = x[..., :rot_dim].astype(jnp.float32)
    x1, x2 = xr[..., :half], xr[..., half:]
    rot = jnp.concatenate([x1 * cos - x2 * sin, x2 * cos + x1 * sin], axis=-1).astype(x.dtype)
    return jnp.concatenate([rot, x[..., rot_dim:]], axis=-1)


def mix_inputs(x, pos, lw):
    B, T = x.shape[:2]
    z = rms_norm(x, lw['g_mix']) @ lw['w_in']
    aq, ak, av, rq, rk, rv, rg = jnp.split(z, SPLITS, axis=-1)
    q = rope(rms_norm(aq.reshape(B, T, A_HEADS, 2, A_DH), lw['g_q']), pos, A_ROT, A_THETA)
    k = rope(rms_norm(ak.reshape(B, T, A_HEADS, 2, A_DH), lw['g_k']), pos, A_ROT, A_THETA)
    q = q.reshape(B, T, A_HEADS, 2 * A_DH)
    k = k.reshape(B, T, A_HEADS, 2 * A_DH)
    v = av.reshape(B, T, A_HEADS, A_DV)
    rq = rope(rq.reshape(B, T, R_HEADS, R_DK), pos, R_DK, R_THETA)
    rk = rope(rk.reshape(B, T, R_HEADS, R_DK), pos, R_DK, R_THETA) * (R_DK ** -0.5)
    rv = rv.reshape(B, T, R_HEADS, R_DV)
    return (q, k, v), (rq, rk, rv, rg)


def diff_lambda(lw, lam_init):
    f = lambda a: a.astype(jnp.float32)
    return (jnp.exp(jnp.sum(f(lw['lam_q1']) * f(lw['lam_k1'])))
            - jnp.exp(jnp.sum(f(lw['lam_q2']) * f(lw['lam_k2']))) + lam_init)


def diff_attend(q, k, v, mask, lam):
    B, T = q.shape[:2]
    L = k.shape[1]
    qc = q.reshape(B, T, A_HEADS, 2, A_DH)
    kc = k.reshape(B, L, A_HEADS, 2, A_DH)
    s = jnp.einsum('bthcd,blhcd->bchtl', qc, kc, preferred_element_type=jnp.float32) * (A_DH ** -0.5)
    if mask is not None:
        s = jnp.where(mask, s, -1e30)
    p = jax.nn.softmax(s, axis=-1)
    a = p[:, 0] - lam * p[:, 1]
    return jnp.einsum('bhtl,blhv->bthv', a.astype(v.dtype), v)


def retention_log_decay():
    return jnp.log(1.0 - 2.0 ** (-5.0 - jnp.arange(R_HEADS, dtype=jnp.float32)))


def retention_state(k, v, S, log_g):
    C = k.shape[1]
    idx = jnp.arange(C, dtype=jnp.float32)
    w = jnp.exp((C - 1 - idx)[:, None] * log_g[None, :])
    return (jnp.exp(C * log_g)[None, :, None, None] * S
            + jnp.einsum('bchk,ch,bchv->bhkv', k, w, v))


def retention_block(q, k, v, S, log_g):
    C = q.shape[1]
    idx = jnp.arange(C, dtype=jnp.float32)
    diff = idx[:, None] - idx[None, :]
    D = jnp.where(diff[..., None] >= 0, jnp.exp(jnp.maximum(diff, 0.0)[..., None] * log_g), 0.0)
    s = jnp.einsum('bnhk,bmhk->bnmh', q, k) * D[None]
    o_in = jnp.einsum('bnmh,bmhv->bnhv', s, v)
    cross = jnp.exp((idx + 1.0)[:, None] * log_g[None, :])
    o_x = jnp.einsum('bnhk,nh,bhkv->bnhv', q, cross, S)
    return o_in + o_x, retention_state(k, v, S, log_g)


def mix_output(x, a_out, r_out, r_gate, lw, lam_init):
    B, T = x.shape[:2]
    a = (rms_norm(a_out, lw['g_sub']) * (1.0 - lam_init)).reshape(B, T, A_HEADS * A_DV)
    r = jax.nn.silu(r_gate) * rms_norm(r_out).reshape(B, T, R_HEADS * R_DV).astype(r_gate.dtype)
    h = x + jnp.concatenate([a.astype(x.dtype), r.astype(x.dtype)], axis=-1) @ lw['w_out']
    hn = rms_norm(h, lw['g_ffn'])
    return h + (jax.nn.silu(hn @ lw['w_gate']) * (hn @ lw['w_up'])) @ lw['w_down']


def prompt_layer(hm, h, lw, lam, lam_init, need_meta):
    B, S = h.shape[:2]
    pos_m = jnp.arange(N_META, dtype=jnp.int32)
    pos_f = N_META + jnp.arange(S, dtype=jnp.int32)
    (aqm, akm, avm), (rqm, rkm, rvm, rgm) = mix_inputs(hm, pos_m, lw)
    (aq, ak, av), (rq, rk, rv, rg) = mix_inputs(h, pos_f, lw)
    k_all = jnp.concatenate([jnp.broadcast_to(akm, (B,) + akm.shape[1:]), ak], axis=1)
    v_all = jnp.concatenate([jnp.broadcast_to(avm, (B,) + avm.shape[1:]), av], axis=1)
    key_chunk = jnp.concatenate([jnp.full((N_META,), -1, jnp.int32), jnp.arange(S, dtype=jnp.int32) // CHUNK])
    nb = S // Q_BLOCK
    q_blocks = jnp.moveaxis(aq.reshape(B, nb, Q_BLOCK, A_HEADS, 2 * A_DH), 1, 0)

    def attend_block(args):
        qb, b = args
        q_chunk = (b * Q_BLOCK + jnp.arange(Q_BLOCK, dtype=jnp.int32)) // CHUNK
        mask = key_chunk[None, :] <= q_chunk[:, None]
        return diff_attend(qb, k_all, v_all, mask, lam)

    ao = lax.map(attend_block, (q_blocks, jnp.arange(nb, dtype=jnp.int32)))
    ao = jnp.moveaxis(ao, 0, 1).reshape(B, S, A_HEADS, A_DV)
    log_g = retention_log_decay()
    s0 = jnp.zeros((1, R_HEADS, R_DK, R_DV), jnp.float32)
    if need_meta:
        rom, s_meta = retention_block(rqm, rkm, rvm, s0, log_g)
    else:
        s_meta = retention_state(rkm, rvm, s0, log_g)
    nc = S // CHUNK

    def to_chunks(t):
        return jnp.moveaxis(t.reshape((B, nc, CHUNK) + t.shape[2:]), 1, 0)

    def step(state, qkv):
        o, state = retention_block(*qkv, state, log_g)
        return state, o

    s_fin, ro = lax.scan(step, jnp.broadcast_to(s_meta, (B,) + s_meta.shape[1:]),
                         (to_chunks(rq), to_chunks(rk), to_chunks(rv)))
    ro = jnp.moveaxis(ro, 0, 1).reshape(B, S, R_HEADS, R_DV)
    h = mix_output(h, ao, ro, rg, lw, lam_init)
    if need_meta:
        aom = diff_attend(aqm, akm, avm, None, lam)
        hm = mix_output(hm, aom, rom, rgm, lw, lam_init)
    return hm, h, k_all, v_all, s_fin


def sample_layer(h, ck, cv, S, lw, lam, lam_init):
    B, T = h.shape[:2]
    pos = N_META + PAST_LEN + jnp.arange(T, dtype=jnp.int32)
    (aq, ak, av), (rq, rk, rv, rg) = mix_inputs(h, pos, lw)
    k_all = jnp.concatenate([ck, ak.astype(ck.dtype)], axis=1)
    v_all = jnp.concatenate([cv, av.astype(cv.dtype)], axis=1)
    ao = diff_attend(aq, k_all, v_all, None, lam)
    ro, s_new = retention_block(rq, rk, rv, S, retention_log_decay())
    h = mix_output(h, ao, ro, rg, lw, lam_init)
    return h, ak, av, s_new


def setup_inputs(seed: int = 0) -> dict:
    key = jax.random.key(seed)
    ks = jax.random.split(key, 24)
    nrm = lambda k, shape, s: jax.random.normal(k, shape, jnp.float32) * s
    return {
        'x_prompt': nrm(ks[0], (BATCH, SEQ, D_MODEL), 1.0),
        'x_sample': nrm(ks[1], (DEC_BATCH, DEC_SEQ, D_MODEL), 1.0),
        'cache_k': nrm(ks[2], (DEPTH, DEC_BATCH, N_META + PAST_LEN, A_HEADS, 2 * A_DH), 1.0),
        'cache_v': nrm(ks[3], (DEPTH, DEC_BATCH, N_META + PAST_LEN, A_HEADS, A_DV), 1.0),
        'state_ret': nrm(ks[4], (DEPTH, DEC_BATCH, R_HEADS, R_DK, R_DV), 0.5),
        'meta': nrm(ks[5], (N_META, D_MODEL), 1.0),
        'g_mix': 1.0 + nrm(ks[6], (DEPTH, D_MODEL), 0.1),
        'w_in': nrm(ks[7], (DEPTH, D_MODEL, P_IN), D_MODEL ** -0.5),
        'g_q': 1.0 + nrm(ks[8], (DEPTH, A_DH), 0.1),
        'g_k': 1.0 + nrm(ks[9], (DEPTH, A_DH), 0.1),
        'lam_q1': nrm(ks[10], (DEPTH, A_DH), 0.1),
        'lam_k1': nrm(ks[11], (DEPTH, A_DH), 0.1),
        'lam_q2': nrm(ks[12], (DEPTH, A_DH), 0.1),
        'lam_k2': nrm(ks[13], (DEPTH, A_DH), 0.1),
        'g_sub': 1.0 + nrm(ks[14], (DEPTH, A_DV), 0.1),
        'w_out': nrm(ks[15], (DEPTH, MIX, D_MODEL), MIX ** -0.5),
        'g_ffn': 1.0 + nrm(ks[16], (DEPTH, D_MODEL), 0.1),
        'w_gate': nrm(ks[17], (DEPTH, D_MODEL, D_FF), D_MODEL ** -0.5),
        'w_up': nrm(ks[18], (DEPTH, D_MODEL, D_FF), D_MODEL ** -0.5),
        'w_down': nrm(ks[19], (DEPTH, D_FF, D_MODEL), D_FF ** -0.5),
    }


def reference(x_prompt, x_sample, cache_k, cache_v, state_ret, meta, g_mix, w_in, g_q, g_k,
              lam_q1, lam_k1, lam_q2, lam_k2, g_sub, w_out, g_ffn, w_gate, w_up, w_down):
    hm = meta[None].astype(x_prompt.dtype)
    hp = x_prompt
    hs = x_sample
    kp, vp, sp, ksn, vsn, ssn = [], [], [], [], [], []
    for l in range(DEPTH):
        lw = dict(g_mix=g_mix[l], w_in=w_in[l], g_q=g_q[l], g_k=g_k[l],
                  lam_q1=lam_q1[l], lam_k1=lam_k1[l], lam_q2=lam_q2[l], lam_k2=lam_k2[l],
                  g_sub=g_sub[l], w_out=w_out[l], g_ffn=g_ffn[l],
                  w_gate=w_gate[l], w_up=w_up[l], w_down=w_down[l])
        lam_init = 0.8 - 0.6 * math.exp(-0.3 * l)
        lam = diff_lambda(lw, lam_init)
        hm, hp, k_l, v_l, s_l = prompt_layer(hm, hp, lw, lam, lam_init, l < DEPTH - 1)
        hs, ks_l, vs_l, ss_l = sample_layer(hs, cache_k[l], cache_v[l], state_ret[l], lw, lam, lam_init)
        kp.append(k_l); vp.append(v_l); sp.append(s_l)
        ksn.append(ks_l); vsn.append(vs_l); ssn.append(ss_l)
    return (hp, hs, jnp.stack(kp), jnp.stack(vp), jnp.stack(sp), jnp.stack(ksn), jnp.stack(vsn), jnp.stack(ssn))
```

```python
import functools
import math

import jax
import jax.numpy as jnp
from jax import lax
from jax.experimental import pallas as pl
from jax.experimental.pallas import tpu as pltpu

F32 = jnp.float32
BF16 = jnp.bfloat16

CHUNK = 64
N_META = 16
EPS = 1e-6
A_HEADS = 4
A_DH = 64
A_DV = 2 * A_DH
A_ROT = A_DH // 4
A_THETA = 500000.0
R_HEADS = 4
R_DK = 64
R_DV = 128
R_THETA = 10000.0
HEAD_W = 128
GROUP_W = A_HEADS * HEAD_W
N_GROUPS = 6
N_TABLES = 9
MASK_VALUE = -1e30

LANES = 128
V7X_VMEM_BYTES = 64 * 1024 * 1024
MXU_TILE = 256


def _vmem_limit(nbytes):
    return int(min(nbytes, V7X_VMEM_BYTES - 8 * 1024 * 1024))


def _dot(a, b):
    return jnp.dot(a, b, preferred_element_type=F32)


def _dot_nt(a, b):
    return lax.dot_general(a, b, (((1,), (1,)), ((), ())), preferred_element_type=F32)


def _dot_tn(a, b):
    return lax.dot_general(a, b, (((0,), (0,)), ((), ())), preferred_element_type=F32)


def _proj_kernel(x_ref, tab_ref, gmix_ref, w_ref, gq_ref, gk_ref, bd_ref,
                 q_ref, kf_ref, kb_ref, vf_ref, vb_ref, rqk_ref, rv_ref, rg_ref):
    x = x_ref[...]
    ms = jnp.mean(x * x, axis=-1, keepdims=True)
    xn = ((x * lax.rsqrt(ms + EPS)) * gmix_ref[...]).astype(BF16)

    def proj(g):
        return _dot(xn, w_ref[:, g * GROUP_W:(g + 1) * GROUP_W])

    def table(i):
        return tab_ref[:, i * HEAD_W:(i + 1) * HEAD_W]

    def seg_norm(z, g):
        zz = (z * z).astype(BF16)
        ss = jnp.concatenate(
            [_dot(zz[:, j * MXU_TILE:(j + 1) * MXU_TILE], bd_ref[...])
             for j in range(GROUP_W // MXU_TILE)], axis=1) * (1.0 / A_DH)
        return (z * lax.rsqrt(ss + EPS)) * g

    def rope(y, t0, shift):
        c, sa, sb = table(t0), table(t0 + 1), table(t0 + 2)
        outs = []
        for j in range(GROUP_W // HEAD_W):
            yj = y[:, j * HEAD_W:(j + 1) * HEAD_W]
            outs.append(yj * c + pltpu.roll(yj, shift, 1) * sa
                        + pltpu.roll(yj, HEAD_W - shift, 1) * sb)
        return jnp.concatenate(outs, axis=1)

    q = rope(seg_norm(proj(0), gq_ref[...]), 0, A_ROT // 2)
    q_ref[...] = q.astype(BF16)
    k = rope(seg_norm(proj(1), gk_ref[...]), 3, A_ROT // 2)
    kf_ref[...] = k
    kb_ref[...] = k.astype(BF16)
    v = proj(2)
    vf_ref[...] = v
    vb_ref[...] = v.astype(BF16)
    rqk_ref[...] = rope(proj(3), 6, R_DK // 2).astype(BF16)
    rv_ref[...] = proj(4).astype(BF16)
    rg_ref[...] = proj(5).astype(BF16)


def _proj(x, tables, gmix, w_in, gq, gk, bd, *, tm):
    B, T, D = x.shape
    assert T % tm == 0
    nt = T // tm
    tok = lambda i, b: (b, i, 0)
    const = lambda i, b: (0, 0)
    act = lambda dt: jax.ShapeDtypeStruct((B, T, GROUP_W), dt)
    act_spec = pl.BlockSpec((None, tm, GROUP_W), tok)
    vmem = (2 * tm * D * 4 + 2 * tm * N_TABLES * HEAD_W * 4 + D * N_GROUPS * GROUP_W * 2
            + 2 * tm * GROUP_W * (6 * 2 + 2 * 4) + 6 * tm * GROUP_W * 4 + tm * D * 6)
    return pl.pallas_call(
        _proj_kernel,
        grid=(nt, B),
        in_specs=[
            pl.BlockSpec((None, tm, D), tok),
            pl.BlockSpec((tm, N_TABLES * HEAD_W), lambda i, b: (i, 0)),
            pl.BlockSpec((1, D), const),
            pl.BlockSpec((D, N_GROUPS * GROUP_W), const, pipeline_mode=pl.Buffered(1)),
            pl.BlockSpec((1, GROUP_W), const),
            pl.BlockSpec((1, GROUP_W), const),
            pl.BlockSpec((MXU_TILE, MXU_TILE), const),
        ],
        out_specs=[act_spec] * 8,
        out_shape=[act(BF16), act(F32), act(BF16), act(F32), act(BF16),
                   act(BF16), act(BF16), act(BF16)],
        compiler_params=pltpu.CompilerParams(
            dimension_semantics=("arbitrary", "arbitrary"),
            vmem_limit_bytes=_vmem_limit(vmem + 8 * 1024 * 1024)),
        name="proj",
    )(x, tables, gmix, w_in, gq, gk, bd)


def _diff_lambda(lam_ref, lam_init):
    lv = lam_ref[...]
    s1 = jnp.sum(lv[0:1] * lv[1:2], axis=-1, keepdims=True)
    s2 = jnp.sum(lv[2:3] * lv[3:4], axis=-1, keepdims=True)
    return jnp.exp(s1) - jnp.exp(s2) + lam_init


def _stack_components(q):
    lane = lax.broadcasted_iota(jnp.int32, q.shape, 1)
    zero = jnp.zeros_like(q)
    return jnp.concatenate([jnp.where(lane < A_DH, q, zero),
                            jnp.where(lane >= A_DH, q, zero)], axis=0)


def _attn_prompt_kernel(lam_ref, q_ref, k_ref, v_ref, km_ref, vm_ref, o_ref,
                        m_sc, l_sc, acc_sc, *, tq, tk, lam_init):
    qi = pl.program_id(2)
    qq = _stack_components(q_ref[...])

    def update(s, vblk):
        m_prev = m_sc[...]
        m_new = jnp.maximum(m_prev, jnp.max(s, axis=-1, keepdims=True))
        alpha = jnp.exp(m_prev - m_new)
        p = jnp.exp(s - m_new)
        l_sc[...] = alpha * l_sc[...] + jnp.sum(p, axis=-1, keepdims=True)
        acc_sc[...] = alpha * acc_sc[...] + _dot(p.astype(BF16), vblk)
        m_sc[...] = m_new

    m_sc[...] = jnp.full(m_sc.shape, MASK_VALUE, F32)
    l_sc[...] = jnp.zeros(l_sc.shape, F32)
    acc_sc[...] = jnp.zeros(acc_sc.shape, F32)

    s = _dot_nt(qq, km_ref[...])
    col = lax.broadcasted_iota(jnp.int32, s.shape, 1)
    update(jnp.where(col < N_META, s, MASK_VALUE), vm_ref[...])

    def body(j, carry):
        start = pl.multiple_of(j * tk, tk)
        update(_dot_nt(qq, k_ref[pl.ds(start, tk), :]), v_ref[pl.ds(start, tk), :])
        return carry

    lax.fori_loop(0, qi * (tq // tk), body, 0)

    row_chunk = (lax.broadcasted_iota(jnp.int32, (2 * tq, tk), 0) % tq) // CHUNK
    col_chunk = lax.broadcasted_iota(jnp.int32, (2 * tq, tk), 1) // CHUNK
    for d in range(tq // tk):
        start = pl.multiple_of(qi * tq + d * tk, tk)
        s = _dot_nt(qq, k_ref[pl.ds(start, tk), :])
        visible = col_chunk + d * (tk // CHUNK) <= row_chunk
        update(jnp.where(visible, s, MASK_VALUE), v_ref[pl.ds(start, tk), :])

    lam = _diff_lambda(lam_ref, lam_init)
    acc = acc_sc[...]
    l = l_sc[...]
    out = acc[:tq] / l[:tq] - lam * (acc[tq:] / l[tq:])
    o_ref[...] = out.astype(o_ref.dtype)


def _attn_prompt(lamv, q, kb, vb, km, vm, *, tq, tk, lam_init):
    B, T, _ = q.shape
    assert T % tq == 0 and tq % tk == 0 and tk % CHUNK == 0
    head_blk = lambda b, h, i: (b, i, h)
    head_all = lambda b, h, i: (b, 0, h)
    meta = lambda b, h, i: (0, h)
    vmem = 2 * 2 * T * HEAD_W * 2 + 6 * 2 * tq * tk * 4 + 8 * tq * HEAD_W * 4
    return pl.pallas_call(
        functools.partial(_attn_prompt_kernel, tq=tq, tk=tk, lam_init=lam_init),
        grid=(B, A_HEADS, T // tq),
        in_specs=[
            pl.BlockSpec(lamv.shape, lambda b, h, i: (0, 0)),
            pl.BlockSpec((None, tq, HEAD_W), head_blk),
            pl.BlockSpec((None, T, HEAD_W), head_all),
            pl.BlockSpec((None, T, HEAD_W), head_all),
            pl.BlockSpec((LANES, HEAD_W), meta),
            pl.BlockSpec((LANES, HEAD_W), meta),
        ],
        out_specs=pl.BlockSpec((None, tq, HEAD_W), head_blk),
        out_shape=jax.ShapeDtypeStruct((B, T, GROUP_W), BF16),
        scratch_shapes=[pltpu.VMEM((2 * tq, 1), F32), pltpu.VMEM((2 * tq, 1), F32),
                        pltpu.VMEM((2 * tq, HEAD_W), F32)],
        compiler_params=pltpu.CompilerParams(
            dimension_semantics=("arbitrary", "arbitrary", "arbitrary"),
            vmem_limit_bytes=_vmem_limit(vmem + 8 * 1024 * 1024)),
        name="attn_prompt",
    )(lamv, q, kb, vb, km, vm)


def _attn_sample_kernel(lam_ref, q_ref, ck_ref, cv_ref, kn_ref, vn_ref, o_ref, *, lam_init):
    t = q_ref.shape[0]
    qq = _stack_components(q_ref[...])
    n_past = ck_ref.shape[0] - N_META
    n_tail = N_META + t
    pad = jnp.zeros((LANES - n_tail, HEAD_W), BF16)

    k_tail = jnp.concatenate([ck_ref[0:N_META, :].astype(BF16), kn_ref[...], pad], axis=0)
    v_tail = jnp.concatenate([cv_ref[0:N_META, :].astype(BF16), vn_ref[...], pad], axis=0)
    s_past = _dot_nt(qq, ck_ref[N_META:, :].astype(BF16))
    s_tail = _dot_nt(qq, k_tail)
    col = lax.broadcasted_iota(jnp.int32, s_tail.shape, 1)
    s_tail = jnp.where(col < n_tail, s_tail, MASK_VALUE)

    m = jnp.maximum(jnp.max(s_past, axis=-1, keepdims=True),
                    jnp.max(s_tail, axis=-1, keepdims=True))
    p_past = jnp.exp(s_past - m)
    p_tail = jnp.exp(s_tail - m)
    l = jnp.sum(p_past, axis=-1, keepdims=True) + jnp.sum(p_tail, axis=-1, keepdims=True)
    acc = (_dot(p_past.astype(BF16), cv_ref[N_META:, :].astype(BF16))
           + _dot(p_tail.astype(BF16), v_tail))
    lam = _diff_lambda(lam_ref, lam_init)
    out = acc[:t] / l[:t] - lam * (acc[t:] / l[t:])
    o_ref[...] = out.astype(o_ref.dtype)
    del n_past


def _attn_sample(lamv, q, ck, cv, kn, vn, *, lam_init):
    B, T, _ = q.shape
    L = ck.shape[1]
    assert N_META + T <= LANES
    head_new = lambda b, h: (b, 0, h)
    vmem = 2 * 2 * L * HEAD_W * 4 + 2 * L * HEAD_W * 2 + 4 * 2 * T * L * 4
    return pl.pallas_call(
        functools.partial(_attn_sample_kernel, lam_init=lam_init),
        grid=(B, A_HEADS),
        in_specs=[
            pl.BlockSpec(lamv.shape, lambda b, h: (0, 0)),
            pl.BlockSpec((None, T, HEAD_W), head_new),
            pl.BlockSpec((None, L, HEAD_W), head_new),
            pl.BlockSpec((None, L, HEAD_W), head_new),
            pl.BlockSpec((None, T, HEAD_W), head_new),
            pl.BlockSpec((None, T, HEAD_W), head_new),
        ],
        out_specs=pl.BlockSpec((None, T, HEAD_W), head_new),
        out_shape=jax.ShapeDtypeStruct((B, T, GROUP_W), BF16),
        compiler_params=pltpu.CompilerParams(
            dimension_semantics=("arbitrary", "arbitrary"),
            vmem_limit_bytes=_vmem_limit(vmem + 8 * 1024 * 1024)),
        name="attn_sample",
    )(lamv, q, ck, cv, kn, vn)


def _ret_kernel(qk_ref, v_ref, s0_ref, dm_ref, cross_ref, wk_ref, gc_ref,
                o_ref, sout_ref, s_sc):
    ci = pl.program_id(2)

    @pl.when(ci == 0)
    def _():
        s_sc[0:R_DK, :] = s0_ref[...]
        s_sc[R_DK:, :] = jnp.zeros((HEAD_W - R_DK, R_DV), F32)

    qk = qk_ref[...]
    lane = lax.broadcasted_iota(jnp.int32, qk.shape, 1)
    qm = jnp.where(lane < R_DK, qk, jnp.zeros_like(qk))
    kf = jnp.where(lane < R_DK, pltpu.roll(qk.astype(F32), R_DK, 1), 0.0)
    v = v_ref[...]

    state = s_sc[...]
    s = _dot_nt(qm, kf.astype(BF16)) * dm_ref[...]
    o = _dot(s.astype(BF16), v) + cross_ref[...] * _dot(qm, state.astype(BF16))
    o_ref[...] = o.astype(o_ref.dtype)

    kw = (kf * wk_ref[...]).astype(BF16)
    new_state = gc_ref[...] * state + _dot_tn(kw, v)
    s_sc[...] = new_state

    @pl.when(ci == pl.num_programs(2) - 1)
    def _():
        sout_ref[...] = new_state[0:R_DK, :]


def _retention_tables(C):
    log_g = jnp.log(1.0 - 2.0 ** (-5.0 - jnp.arange(R_HEADS, dtype=F32)))
    idx = jnp.arange(C, dtype=F32)
    diff = idx[:, None] - idx[None, :]
    dm = jnp.where(diff[None] >= 0,
                   jnp.exp(jnp.maximum(diff, 0.0)[None] * log_g[:, None, None]), 0.0)
    cross = jnp.exp((idx + 1.0)[None, :] * log_g[:, None])
    wk = jnp.exp((C - 1 - idx)[None, :] * log_g[:, None])
    gc = jnp.exp(C * log_g)
    bc = lambda a: jnp.broadcast_to(a[..., None], a.shape + (HEAD_W,))
    return dm, bc(cross), bc(wk), bc(gc)[:, None, :]


def _retention(rqk, rv, s0, *, C):
    B, T, _ = rqk.shape
    assert T % C == 0
    dm, cross, wk, gc = _retention_tables(C)
    blk = lambda b, h, c: (b, c, h)
    per_head = lambda b, h, c: (h, 0, 0)
    if s0.shape[0] == 1:
        s0_map = lambda b, h, c: (0, h, 0, 0)
    else:
        s0_map = lambda b, h, c: (b, h, 0, 0)
    return pl.pallas_call(
        _ret_kernel,
        grid=(B, R_HEADS, T // C),
        in_specs=[
            pl.BlockSpec((None, C, HEAD_W), blk),
            pl.BlockSpec((None, C, HEAD_W), blk),
            pl.BlockSpec((None, None, R_DK, R_DV), s0_map),
            pl.BlockSpec((None, C, C), per_head),
            pl.BlockSpec((None, C, HEAD_W), per_head),
            pl.BlockSpec((None, C, HEAD_W), per_head),
            pl.BlockSpec((None, 1, HEAD_W), per_head),
        ],
        out_specs=[
            pl.BlockSpec((None, C, HEAD_W), blk),
            pl.BlockSpec((None, None, R_DK, R_DV), lambda b, h, c: (b, h, 0, 0)),
        ],
        out_shape=[jax.ShapeDtypeStruct((B, T, GROUP_W), BF16),
                   jax.ShapeDtypeStruct((B, R_HEADS, R_DK, R_DV), F32)],
        scratch_shapes=[pltpu.VMEM((HEAD_W, R_DV), F32)],
        compiler_params=pltpu.CompilerParams(
            dimension_semantics=("arbitrary", "arbitrary", "arbitrary")),
        name="retention",
    )(rqk, rv, s0, dm, cross, wk, gc)


def _silu(x):
    return x * (1.0 / (1.0 + jnp.exp(-x)))


def _out_kernel(x_ref, ao_ref, ro_ref, rg_ref, gsub_ref, wout_ref, gffn_ref,
                wg_ref, wu_ref, wd_ref, y_ref, *, lam_init, ffc):
    def head_norm(t):
        outs = []
        for j in range(GROUP_W // HEAD_W):
            tj = t[:, j * HEAD_W:(j + 1) * HEAD_W]
            outs.append(tj * lax.rsqrt(jnp.mean(tj * tj, axis=-1, keepdims=True) + EPS))
        return jnp.concatenate(outs, axis=1)

    a = (head_norm(ao_ref[...].astype(F32)) * gsub_ref[...]) * (1.0 - lam_init)
    r = _silu(rg_ref[...].astype(F32)) * head_norm(ro_ref[...].astype(F32))
    mix = jnp.concatenate([a, r], axis=1).astype(BF16)
    h = x_ref[...] + _dot(mix, wout_ref[...])
    hn = ((h * lax.rsqrt(jnp.mean(h * h, axis=-1, keepdims=True) + EPS))
          * gffn_ref[...]).astype(BF16)
    y = h
    for c in range(wg_ref.shape[1] // ffc):
        gate = _dot(hn, wg_ref[:, c * ffc:(c + 1) * ffc])
        up = _dot(hn, wu_ref[:, c * ffc:(c + 1) * ffc])
        y = y + _dot((_silu(gate) * up).astype(BF16), wd_ref[c * ffc:(c + 1) * ffc, :])
    y_ref[...] = y


def _out(x, ao, ro, rg, gsub, wout, gffn, wg, wu, wd, *, tm, ffc, lam_init):
    N, D = x.shape
    F = wg.shape[1]
    assert N % tm == 0 and F % ffc == 0 and ffc % LANES == 0
    tok = lambda i: (i, 0)
    const = lambda i: (0, 0)
    resident = lambda shape: pl.BlockSpec(shape, const, pipeline_mode=pl.Buffered(1))
    vmem = ((D * D + 3 * D * F) * 2 + 2 * 2 * tm * D * 4 + 2 * 3 * tm * GROUP_W * 2
            + 5 * tm * D * 4 + 3 * tm * ffc * 4)
    return pl.pallas_call(
        functools.partial(_out_kernel, lam_init=lam_init, ffc=ffc),
        grid=(N // tm,),
        in_specs=[
            pl.BlockSpec((tm, D), tok),
            pl.BlockSpec((tm, GROUP_W), tok),
            pl.BlockSpec((tm, GROUP_W), tok),
            pl.BlockSpec((tm, GROUP_W), tok),
            pl.BlockSpec((1, GROUP_W), const),
            resident((D, D)),
            pl.BlockSpec((1, D), const),
            resident((D, F)),
            resident((D, F)),
            resident((F, D)),
        ],
        out_specs=pl.BlockSpec((tm, D), tok),
        out_shape=jax.ShapeDtypeStruct((N, D), F32),
        compiler_params=pltpu.CompilerParams(
            dimension_semantics=("arbitrary",),
            vmem_limit_bytes=_vmem_limit(vmem + 8 * 1024 * 1024)),
        name="out_ffn",
    )(x, ao, ro, rg, gsub, wout, gffn, wg, wu, wd)


def _rope_tables(pos):
    posf = pos.astype(F32)
    T = pos.shape[0]

    def seg_tables(rot_dim, theta):
        half = rot_dim // 2
        inv = theta ** (-jnp.arange(half, dtype=F32) / half)
        ang = posf[:, None] * inv[None, :]
        cos, sin = jnp.cos(ang), jnp.sin(ang)
        rest = A_DH - rot_dim
        zh = jnp.zeros((T, half), F32)
        c = jnp.concatenate([cos, cos, jnp.ones((T, rest), F32)], axis=1)
        sa = jnp.concatenate([zh, sin, jnp.zeros((T, rest), F32)], axis=1)
        sb = jnp.concatenate([-sin, zh, jnp.zeros((T, rest), F32)], axis=1)
        return c, sa, sb

    att = seg_tables(A_ROT, A_THETA)
    ret = seg_tables(R_DK, R_THETA)
    scale = A_DH ** -0.5
    tabs = ([jnp.concatenate([t, t], axis=1) * scale for t in att]
            + [jnp.concatenate([t, t], axis=1) for t in att]
            + [jnp.concatenate([t, t * (R_DK ** -0.5)], axis=1) for t in ret])
    return jnp.concatenate(tabs, axis=1)


def _permute_w_in(w_in):
    a = 3 * GROUP_W
    rq0, rk0 = a, a + R_HEADS * R_DK
    h = jnp.arange(R_HEADS)[:, None]
    d = jnp.arange(R_DK)[None, :]
    rqk = jnp.concatenate([rq0 + h * R_DK + d, rk0 + h * R_DK + d], axis=1).reshape(-1)
    rest = jnp.arange(rk0 + R_HEADS * R_DK, w_in.shape[1])
    cols = jnp.concatenate([jnp.arange(a), rqk, rest])
    return w_in[:, cols]


def kernel(x_prompt, x_sample, cache_k, cache_v, state_ret, meta, g_mix, w_in, g_q, g_k,
           lam_q1, lam_k1, lam_q2, lam_k2, g_sub, w_out, g_ffn, w_gate, w_up, w_down):
    depth = cache_k.shape[0]
    assert depth == 1, "single-layer stack only"
    B, S, D = x_prompt.shape
    Bd, Td, _ = x_sample.shape
    L_past = cache_k.shape[2]
    lam_init = 0.8 - 0.6 * math.exp(-0.3 * 0)

    w_in_b = _permute_w_in(w_in[0]).astype(BF16)
    gmix = g_mix[0][None, :]
    gq = jnp.tile(g_q[0], GROUP_W // A_DH)[None, :]
    gk = jnp.tile(g_k[0], GROUP_W // A_DH)[None, :]
    gsub = jnp.tile(g_sub[0], A_HEADS)[None, :]
    gffn = g_ffn[0][None, :]
    lamv = jnp.stack([lam_q1[0], lam_k1[0], lam_q2[0], lam_k2[0]])
    seg = jnp.arange(MXU_TILE) // A_DH
    bd = (seg[:, None] == seg[None, :]).astype(BF16)
    wout_b, wg_b, wu_b, wd_b = (w[0].astype(BF16) for w in (w_out, w_gate, w_up, w_down))

    proj = functools.partial(_proj, gmix=gmix, w_in=w_in_b, gq=gq, gk=gk, bd=bd)
    pos_m = jnp.arange(N_META, dtype=jnp.int32)
    pos_f = N_META + jnp.arange(S, dtype=jnp.int32)
    pos_s = L_past + jnp.arange(Td, dtype=jnp.int32)

    _, kf_m, kb_m, vf_m, vb_m, rqk_m, rv_m, _ = proj(
        meta[None].astype(x_prompt.dtype), _rope_tables(pos_m), tm=N_META)
    q_p, kf_p, kb_p, vf_p, vb_p, rqk_p, rv_p, rg_p = proj(
        x_prompt, _rope_tables(pos_f), tm=min(S, 512))
    q_s, kf_s, kb_s, vf_s, vb_s, rqk_s, rv_s, rg_s = proj(
        x_sample, _rope_tables(pos_s), tm=Td)

    pad_meta = lambda t: jnp.pad(t[0], ((0, LANES - N_META), (0, 0)))
    tq = min(S, 256)
    ao_p = _attn_prompt(lamv, q_p, kb_p, vb_p, pad_meta(kb_m), pad_meta(vb_m),
                        tq=tq, tk=tq, lam_init=lam_init)
    ck = cache_k[0].reshape(Bd, L_past, GROUP_W)
    cv = cache_v[0].reshape(Bd, L_past, GROUP_W)
    ao_s = _attn_sample(lamv, q_s, ck, cv, kb_s, vb_s, lam_init=lam_init)

    zero_state = jnp.zeros((1, R_HEADS, R_DK, R_DV), F32)
    _, s_meta = _retention(rqk_m, rv_m, zero_state, C=N_META)
    ro_p, s_fin = _retention(rqk_p, rv_p, s_meta, C=min(S, 256))
    ro_s, s_new = _retention(rqk_s, rv_s, state_ret[0], C=Td)

    out = functools.partial(_out, gsub=gsub, wout=wout_b, gffn=gffn, wg=wg_b, wu=wu_b,
                            wd=wd_b, ffc=w_gate.shape[2] // 2, lam_init=lam_init)
    flat = lambda t: t.reshape(-1, t.shape[-1])
    y_p = out(flat(x_prompt), flat(ao_p), flat(ro_p), flat(rg_p),
              tm=min(B * S, 512)).reshape(B, S, D)
    y_s = out(flat(x_sample), flat(ao_s), flat(ro_s), flat(rg_s),
              tm=min(Bd * Td, 512)).reshape(Bd, Td, D)

    def with_meta(meta_rows, frames):
        rows = jnp.broadcast_to(meta_rows, (B,) + meta_rows.shape[1:])
        full = jnp.concatenate([rows, frames], axis=1)
        return full.reshape(1, B, N_META + S, A_HEADS, HEAD_W)

    per_head = lambda t: t.reshape(1, Bd, Td, A_HEADS, HEAD_W)
    return (y_p, y_s, with_meta(kf_m, kf_p), with_meta(vf_m, vf_p), s_fin[None],
            per_head(kf_s), per_head(vf_s), s_new[None])
```

```python
import functools
import math

import jax
import jax.numpy as jnp
from jax import lax
from jax.experimental import pallas as pl
from jax.experimental.pallas import tpu as pltpu

F32 = jnp.float32
BF16 = jnp.bfloat16

CHUNK = 64
N_META = 16
EPS = 1e-6
A_HEADS = 4
A_DH = 64
A_DV = 2 * A_DH
A_ROT = A_DH // 4
A_THETA = 500000.0
R_HEADS = 4
R_DK = 64
R_DV = 128
R_THETA = 10000.0
HEAD_W = 128
GROUP_W = A_HEADS * HEAD_W
N_GROUPS = 6
N_TABLES = 9
MASK_VALUE = -1e30
KEY_BLOCK = 256
LOG2E = 1.4426950408889634

LANES = 128
V7X_VMEM_BYTES = 64 * 1024 * 1024
MXU_TILE = 256


def _vmem_limit(nbytes):
    return int(min(nbytes, V7X_VMEM_BYTES - 8 * 1024 * 1024))


def _dot(a, b):
    return jnp.dot(a, b, preferred_element_type=F32)


def _dot_nt(a, b):
    return lax.dot_general(a, b, (((1,), (1,)), ((), ())), preferred_element_type=F32)


def _dot_tn(a, b):
    return lax.dot_general(a, b, (((0,), (0,)), ((), ())), preferred_element_type=F32)


def _proj_kernel(x_ref, tab_ref, gmix_ref, w_ref, gq_ref, gk_ref, bd_ref,
                 q_ref, kf_ref, kb_ref, vf_ref, vb_ref, rqk_ref, rv_ref, rg_ref, *, v_transposed):
    x = x_ref[...]
    ms = jnp.mean(x * x, axis=-1, keepdims=True)
    xn = ((x * lax.rsqrt(ms + EPS)) * gmix_ref[...]).astype(BF16)

    def proj(g):
        return _dot(xn, w_ref[:, g * GROUP_W:(g + 1) * GROUP_W])

    def table(i):
        return tab_ref[:, i * HEAD_W:(i + 1) * HEAD_W]

    def seg_norm(z, g):
        zz = (z * z).astype(BF16)
        ss = jnp.concatenate(
            [_dot(zz[:, j * MXU_TILE:(j + 1) * MXU_TILE], bd_ref[...])
             for j in range(GROUP_W // MXU_TILE)], axis=1) * (1.0 / A_DH)
        return (z * lax.rsqrt(ss + EPS)) * g

    def rope(y, t0, shift):
        c, sa, sb = table(t0), table(t0 + 1), table(t0 + 2)
        outs = []
        for j in range(GROUP_W // HEAD_W):
            yj = y[:, j * HEAD_W:(j + 1) * HEAD_W]
            outs.append(yj * c + pltpu.roll(yj, shift, 1) * sa
                        + pltpu.roll(yj, HEAD_W - shift, 1) * sb)
        return jnp.concatenate(outs, axis=1)

    q = rope(seg_norm(proj(0), gq_ref[...]), 0, A_ROT // 2)
    q_ref[...] = q.astype(BF16)
    k = rope(seg_norm(proj(1), gk_ref[...]), 3, A_ROT // 2)
    kf_ref[...] = k
    kb_ref[...] = k.astype(BF16)
    v = proj(2)
    vf_ref[...] = v
    if v_transposed:
        vt = v.T
        for h in range(A_HEADS):
            for j in range(vb_ref.shape[1]):
                vb_ref[h, j] = vt[h * HEAD_W:(h + 1) * HEAD_W,
                                  j * KEY_BLOCK:(j + 1) * KEY_BLOCK].astype(BF16)
    else:
        vb_ref[...] = v.astype(BF16)
    rqk_ref[...] = rope(proj(3), 6, R_DK // 2).astype(BF16)
    rv_ref[...] = proj(4).astype(BF16)
    rg_ref[...] = proj(5).astype(BF16)


def _proj(x, tables, gmix, w_in, gq, gk, bd, *, tm, v_transposed=False):
    B, T, D = x.shape
    assert T % tm == 0
    nt = T // tm
    tok = lambda i, b: (b, i, 0)
    const = lambda i, b: (0, 0)
    act = lambda dt: jax.ShapeDtypeStruct((B, T, GROUP_W), dt)
    act_spec = pl.BlockSpec((None, tm, GROUP_W), tok)
    vmem = (2 * tm * D * 4 + 2 * tm * N_TABLES * HEAD_W * 4 + D * N_GROUPS * GROUP_W * 2
            + 2 * tm * GROUP_W * (6 * 2 + 2 * 4) + 6 * tm * GROUP_W * 4 + tm * D * 6)
    out_specs = [act_spec] * 8
    out_shape = [act(BF16), act(F32), act(BF16), act(F32), act(BF16),
                 act(BF16), act(BF16), act(BF16)]
    if v_transposed:
        assert tm % KEY_BLOCK == 0
        out_specs[4] = pl.BlockSpec((None, A_HEADS, tm // KEY_BLOCK, HEAD_W, KEY_BLOCK),
                                    lambda i, b: (b, 0, i, 0, 0))
        out_shape[4] = jax.ShapeDtypeStruct(
            (B, A_HEADS, T // KEY_BLOCK, HEAD_W, KEY_BLOCK), BF16)
    return pl.pallas_call(
        functools.partial(_proj_kernel, v_transposed=v_transposed),
        grid=(nt, B),
        in_specs=[
            pl.BlockSpec((None, tm, D), tok),
            pl.BlockSpec((tm, N_TABLES * HEAD_W), lambda i, b: (i, 0)),
            pl.BlockSpec((1, D), const),
            pl.BlockSpec((D, N_GROUPS * GROUP_W), const, pipeline_mode=pl.Buffered(1)),
            pl.BlockSpec((1, GROUP_W), const),
            pl.BlockSpec((1, GROUP_W), const),
            pl.BlockSpec((MXU_TILE, MXU_TILE), const),
        ],
        out_specs=out_specs,
        out_shape=out_shape,
        compiler_params=pltpu.CompilerParams(
            dimension_semantics=("arbitrary", "arbitrary"),
            vmem_limit_bytes=_vmem_limit(vmem + 8 * 1024 * 1024)),
        name="proj",
    )(x, tables, gmix, w_in, gq, gk, bd)


def _diff_lambda(lam_ref, lam_init):
    lv = lam_ref[...]
    s1 = jnp.sum(lv[0:1] * lv[1:2], axis=-1, keepdims=True)
    s2 = jnp.sum(lv[2:3] * lv[3:4], axis=-1, keepdims=True)
    return jnp.exp(s1) - jnp.exp(s2) + lam_init


def _stack_components(q):
    lane = lax.broadcasted_iota(jnp.int32, q.shape, 1)
    zero = jnp.zeros_like(q)
    return jnp.concatenate([jnp.where(lane < A_DH, q, zero),
                            jnp.where(lane >= A_DH, q, zero)], axis=0)


def _attn_prompt_kernel(lam_ref, q_ref, k_ref, vt_ref, km_ref, vmt_ref, o_ref,
                        qt_sc, m_sc, l_sc, acc_sc, *, tq, lam_init):
    qi = pl.program_id(2)
    tk = KEY_BLOCK

    qt = q_ref[...].astype(F32).T
    feat = lax.broadcasted_iota(jnp.int32, qt.shape, 0)
    qt_sc[0] = jnp.where(feat < A_DH, qt, 0.0).astype(BF16)
    qt_sc[1] = jnp.where(feat >= A_DH, qt, 0.0).astype(BF16)

    def start_state(c, s, vt):
        m = jnp.max(s, axis=0, keepdims=True)
        p = jnp.exp2(s - m)
        m_sc[c] = m
        l_sc[c] = jnp.sum(p, axis=0, keepdims=True)
        acc_sc[c] = _dot(vt, p.astype(BF16))

    def update(c, s, vt):
        m_prev = m_sc[c]
        m_new = jnp.maximum(m_prev, jnp.max(s, axis=0, keepdims=True))
        alpha = jnp.exp2(m_prev - m_new)
        p = jnp.exp2(s - m_new)
        l_sc[c] = alpha * l_sc[c] + jnp.sum(p, axis=0, keepdims=True)
        acc_sc[c] = alpha * acc_sc[c] + _dot(vt, p.astype(BF16))
        m_sc[c] = m_new

    key = lax.broadcasted_iota(jnp.int32, (LANES, tq), 0)
    for c in range(2):
        s = _dot(km_ref[...], qt_sc[c])
        start_state(c, jnp.where(key < N_META, s, MASK_VALUE), vmt_ref[...])

    def block(j, visible=None):
        kblk = k_ref[pl.ds(pl.multiple_of(j * tk, tk), tk), :]
        vt = vt_ref[j]
        for c in range(2):
            s = _dot(kblk, qt_sc[c])
            if visible is not None:
                s = jnp.where(visible, s, MASK_VALUE)
            update(c, s, vt)

    def block_pair(j0):
        kb = [k_ref[pl.ds(pl.multiple_of((j0 + i) * tk, tk), tk), :] for i in range(2)]
        vt = [vt_ref[j0 + i] for i in range(2)]
        rhs = [qt_sc[c] for c in range(2)]
        state = [(m_sc[c], l_sc[c], acc_sc[c]) for c in range(2)]
        s = [[_dot(kb[i], rhs[c]) for i in range(2)] for c in range(2)]
        for c in range(2):
            m_prev, l_prev, acc_prev = state[c]
            m_new = jnp.maximum(m_prev, jnp.maximum(jnp.max(s[c][0], axis=0, keepdims=True),
                                                    jnp.max(s[c][1], axis=0, keepdims=True)))
            alpha = jnp.exp2(m_prev - m_new)
            p = [jnp.exp2(s[c][i] - m_new) for i in range(2)]
            l_sc[c] = (alpha * l_prev + jnp.sum(p[0], axis=0, keepdims=True)
                       + jnp.sum(p[1], axis=0, keepdims=True))
            acc_sc[c] = (alpha * acc_prev + _dot(vt[0], p[0].astype(BF16))
                         + _dot(vt[1], p[1].astype(BF16)))
            m_sc[c] = m_new

    def body(jj, carry):
        block_pair(2 * jj)
        return carry

    n_full = qi * (tq // tk)
    lax.fori_loop(0, n_full // 2, body, 0)

    @pl.when(n_full % 2 == 1)
    def _():
        block(n_full - 1)

    key_chunk = lax.broadcasted_iota(jnp.int32, (tk, tq), 0) // CHUNK
    qry_chunk = lax.broadcasted_iota(jnp.int32, (tk, tq), 1) // CHUNK
    for d in range(tq // tk):
        block(n_full + d, key_chunk + d * (tk // CHUNK) <= qry_chunk)

    lam = _diff_lambda(lam_ref, lam_init)
    out_t = acc_sc[0] / l_sc[0] - lam * (acc_sc[1] / l_sc[1])
    o_ref[...] = out_t.T.astype(o_ref.dtype)


def _attn_prompt(lamv, q, kb, vt, km, vmt, *, tq, lam_init):
    B, T, _ = q.shape
    assert T % tq == 0 and tq % KEY_BLOCK == 0 and KEY_BLOCK % CHUNK == 0
    nkb = T // KEY_BLOCK
    head_blk = lambda b, h, i: (b, i, h)
    vmem = (2 * T * HEAD_W * 2 * 2 + 8 * KEY_BLOCK * tq * 4 * 2 + 8 * tq * HEAD_W * 4)
    return pl.pallas_call(
        functools.partial(_attn_prompt_kernel, tq=tq, lam_init=lam_init),
        grid=(B, A_HEADS, T // tq),
        in_specs=[
            pl.BlockSpec(lamv.shape, lambda b, h, i: (0, 0)),
            pl.BlockSpec((None, tq, HEAD_W), head_blk),
            pl.BlockSpec((None, T, HEAD_W), lambda b, h, i: (b, 0, h)),
            pl.BlockSpec((None, None, nkb, HEAD_W, KEY_BLOCK), lambda b, h, i: (b, h, 0, 0, 0)),
            pl.BlockSpec((None, LANES, HEAD_W), lambda b, h, i: (h, 0, 0)),
            pl.BlockSpec((None, HEAD_W, LANES), lambda b, h, i: (h, 0, 0)),
        ],
        out_specs=pl.BlockSpec((None, tq, HEAD_W), head_blk),
        out_shape=jax.ShapeDtypeStruct((B, T, GROUP_W), BF16),
        scratch_shapes=[pltpu.VMEM((2, HEAD_W, tq), BF16),
                        pltpu.VMEM((2, 1, tq), F32), pltpu.VMEM((2, 1, tq), F32),
                        pltpu.VMEM((2, HEAD_W, tq), F32)],
        compiler_params=pltpu.CompilerParams(
            dimension_semantics=("arbitrary", "arbitrary", "arbitrary"),
            vmem_limit_bytes=_vmem_limit(vmem + 8 * 1024 * 1024)),
        name="attn_prompt",
    )(lamv, q, kb, vt, km, vmt)


def _attn_sample_kernel(lam_ref, q_ref, ck_ref, cv_ref, kn_ref, vn_ref, o_ref, *, lam_init):
    t = q_ref.shape[0]
    qq = _stack_components(q_ref[...])
    n_past = ck_ref.shape[0] - N_META
    n_tail = N_META + t
    pad = jnp.zeros((LANES - n_tail, HEAD_W), BF16)

    k_tail = jnp.concatenate([ck_ref[0:N_META, :].astype(BF16), kn_ref[...], pad], axis=0)
    v_tail = jnp.concatenate([cv_ref[0:N_META, :].astype(BF16), vn_ref[...], pad], axis=0)
    s_past = _dot_nt(qq, ck_ref[N_META:, :].astype(BF16))
    s_tail = _dot_nt(qq, k_tail)
    col = lax.broadcasted_iota(jnp.int32, s_tail.shape, 1)
    s_tail = jnp.where(col < n_tail, s_tail, MASK_VALUE)

    m = jnp.maximum(jnp.max(s_past, axis=-1, keepdims=True),
                    jnp.max(s_tail, axis=-1, keepdims=True))
    p_past = jnp.exp2(s_past - m)
    p_tail = jnp.exp2(s_tail - m)
    l = jnp.sum(p_past, axis=-1, keepdims=True) + jnp.sum(p_tail, axis=-1, keepdims=True)
    acc = (_dot(p_past.astype(BF16), cv_ref[N_META:, :].astype(BF16))
           + _dot(p_tail.astype(BF16), v_tail))
    lam = _diff_lambda(lam_ref, lam_init)
    out = acc[:t] / l[:t] - lam * (acc[t:] / l[t:])
    o_ref[...] = out.astype(o_ref.dtype)
    del n_past


def _attn_sample(lamv, q, ck, cv, kn, vn, *, lam_init):
    B, T, _ = q.shape
    L = ck.shape[1]
    assert N_META + T <= LANES
    head_new = lambda b, h: (b, 0, h)
    vmem = 2 * 2 * L * HEAD_W * 4 + 2 * L * HEAD_W * 2 + 4 * 2 * T * L * 4
    return pl.pallas_call(
        functools.partial(_attn_sample_kernel, lam_init=lam_init),
        grid=(B, A_HEADS),
        in_specs=[
            pl.BlockSpec(lamv.shape, lambda b, h: (0, 0)),
            pl.BlockSpec((None, T, HEAD_W), head_new),
            pl.BlockSpec((None, L, HEAD_W), head_new),
            pl.BlockSpec((None, L, HEAD_W), head_new),
            pl.BlockSpec((None, T, HEAD_W), head_new),
            pl.BlockSpec((None, T, HEAD_W), head_new),
        ],
        out_specs=pl.BlockSpec((None, T, HEAD_W), head_new),
        out_shape=jax.ShapeDtypeStruct((B, T, GROUP_W), BF16),
        compiler_params=pltpu.CompilerParams(
            dimension_semantics=("arbitrary", "arbitrary"),
            vmem_limit_bytes=_vmem_limit(vmem + 8 * 1024 * 1024)),
        name="attn_sample",
    )(lamv, q, ck, cv, kn, vn)


def _ret_kernel(qk_ref, v_ref, s0_ref, dm_ref, cross_ref, wk_ref, gc_ref,
                o_ref, sout_ref, s_sc):
    ci = pl.program_id(1)

    @pl.when(ci == 0)
    def _():
        for h in range(R_HEADS):
            s_sc[h, 0:R_DK, :] = s0_ref[h]
            s_sc[h, R_DK:, :] = jnp.zeros((HEAD_W - R_DK, R_DV), F32)

    for h in range(R_HEADS):
        cols = slice(h * HEAD_W, (h + 1) * HEAD_W)
        qk = qk_ref[:, cols]
        lane = lax.broadcasted_iota(jnp.int32, qk.shape, 1)
        qm = jnp.where(lane < R_DK, qk, jnp.zeros_like(qk))
        kf = jnp.where(lane < R_DK, pltpu.roll(qk.astype(F32), R_DK, 1), 0.0)
        v = v_ref[:, cols]

        state = s_sc[h]
        s = _dot_nt(qm, kf.astype(BF16)) * dm_ref[h]
        o = _dot(s.astype(BF16), v) + cross_ref[h] * _dot(qm, state.astype(BF16))
        o_ref[:, cols] = o.astype(o_ref.dtype)

        kw = (kf * wk_ref[h]).astype(BF16)
        new_state = gc_ref[h] * state + _dot_tn(kw, v)
        s_sc[h] = new_state

        @pl.when(ci == pl.num_programs(1) - 1)
        def _():
            sout_ref[h] = new_state[0:R_DK, :]


def _retention_tables(C):
    log_g = jnp.log(1.0 - 2.0 ** (-5.0 - jnp.arange(R_HEADS, dtype=F32)))
    idx = jnp.arange(C, dtype=F32)
    diff = idx[:, None] - idx[None, :]
    dm = jnp.where(diff[None] >= 0,
                   jnp.exp(jnp.maximum(diff, 0.0)[None] * log_g[:, None, None]), 0.0)
    cross = jnp.exp((idx + 1.0)[None, :] * log_g[:, None])
    wk = jnp.exp((C - 1 - idx)[None, :] * log_g[:, None])
    gc = jnp.exp(C * log_g)
    bc = lambda a: jnp.broadcast_to(a[..., None], a.shape + (HEAD_W,))
    return dm, bc(cross), bc(wk), bc(gc)[:, None, :]


def _retention(rqk, rv, s0, *, C):
    B, T, _ = rqk.shape
    assert T % C == 0
    dm, cross, wk, gc = _retention_tables(C)
    blk = lambda b, c: (b, c, 0)
    const = lambda b, c: (0, 0, 0)
    if s0.shape[0] == 1:
        s0_map = lambda b, c: (0, 0, 0, 0)
    else:
        s0_map = lambda b, c: (b, 0, 0, 0)
    return pl.pallas_call(
        _ret_kernel,
        grid=(B, T // C),
        in_specs=[
            pl.BlockSpec((None, C, GROUP_W), blk),
            pl.BlockSpec((None, C, GROUP_W), blk),
            pl.BlockSpec((None, R_HEADS, R_DK, R_DV), s0_map),
            pl.BlockSpec((R_HEADS, C, C), const),
            pl.BlockSpec((R_HEADS, C, HEAD_W), const),
            pl.BlockSpec((R_HEADS, C, HEAD_W), const),
            pl.BlockSpec((R_HEADS, 1, HEAD_W), const),
        ],
        out_specs=[
            pl.BlockSpec((None, C, GROUP_W), blk),
            pl.BlockSpec((None, R_HEADS, R_DK, R_DV), lambda b, c: (b, 0, 0, 0)),
        ],
        out_shape=[jax.ShapeDtypeStruct((B, T, GROUP_W), BF16),
                   jax.ShapeDtypeStruct((B, R_HEADS, R_DK, R_DV), F32)],
        scratch_shapes=[pltpu.VMEM((R_HEADS, HEAD_W, R_DV), F32)],
        compiler_params=pltpu.CompilerParams(
            dimension_semantics=("arbitrary", "arbitrary")),
        name="retention",
    )(rqk, rv, s0, dm, cross, wk, gc)


def _silu(x):
    return x * (1.0 / (1.0 + jnp.exp(-x)))


def _out_kernel(x_ref, ao_ref, ro_ref, rg_ref, gsub_ref, wout_ref, gffn_ref,
                wg_ref, wu_ref, wd_ref, y_ref, *, lam_init, ffc):
    def head_norm(t):
        outs = []
        for j in range(GROUP_W // HEAD_W):
            tj = t[:, j * HEAD_W:(j + 1) * HEAD_W]
            outs.append(tj * lax.rsqrt(jnp.mean(tj * tj, axis=-1, keepdims=True) + EPS))
        return jnp.concatenate(outs, axis=1)

    a = (head_norm(ao_ref[...].astype(F32)) * gsub_ref[...]) * (1.0 - lam_init)
    r = _silu(rg_ref[...].astype(F32)) * head_norm(ro_ref[...].astype(F32))
    mix = jnp.concatenate([a, r], axis=1).astype(BF16)
    h = x_ref[...] + _dot(mix, wout_ref[...])
    hn = ((h * lax.rsqrt(jnp.mean(h * h, axis=-1, keepdims=True) + EPS))
          * gffn_ref[...]).astype(BF16)
    y = h
    for c in range(wg_ref.shape[1] // ffc):
        gate = _dot(hn, wg_ref[:, c * ffc:(c + 1) * ffc])
        up = _dot(hn, wu_ref[:, c * ffc:(c + 1) * ffc])
        y = y + _dot((_silu(gate) * up).astype(BF16), wd_ref[c * ffc:(c + 1) * ffc, :])
    y_ref[...] = y


def _out(x, ao, ro, rg, gsub, wout, gffn, wg, wu, wd, *, tm, ffc, lam_init):
    N, D = x.shape
    F = wg.shape[1]
    assert N % tm == 0 and F % ffc == 0 and ffc % LANES == 0
    tok = lambda i: (i, 0)
    const = lambda i: (0, 0)
    resident = lambda shape: pl.BlockSpec(shape, const, pipeline_mode=pl.Buffered(1))
    vmem = ((D * D + 3 * D * F) * 2 + 2 * 2 * tm * D * 4 + 2 * 3 * tm * GROUP_W * 2
            + 5 * tm * D * 4 + 3 * tm * ffc * 4)
    return pl.pallas_call(
        functools.partial(_out_kernel, lam_init=lam_init, ffc=ffc),
        grid=(N // tm,),
        in_specs=[
            pl.BlockSpec((tm, D), tok),
            pl.BlockSpec((tm, GROUP_W), tok),
            pl.BlockSpec((tm, GROUP_W), tok),
            pl.BlockSpec((tm, GROUP_W), tok),
            pl.BlockSpec((1, GROUP_W), const),
            resident((D, D)),
            pl.BlockSpec((1, D), const),
            resident((D, F)),
            resident((D, F)),
            resident((F, D)),
        ],
        out_specs=pl.BlockSpec((tm, D), tok),
        out_shape=jax.ShapeDtypeStruct((N, D), F32),
        compiler_params=pltpu.CompilerParams(
            dimension_semantics=("arbitrary",),
            vmem_limit_bytes=_vmem_limit(vmem + 8 * 1024 * 1024)),
        name="out_ffn",
    )(x, ao, ro, rg, gsub, wout, gffn, wg, wu, wd)


def _rope_tables(pos):
    posf = pos.astype(F32)
    T = pos.shape[0]

    def seg_tables(rot_dim, theta):
        half = rot_dim // 2
        inv = theta ** (-jnp.arange(half, dtype=F32) / half)
        ang = posf[:, None] * inv[None, :]
        cos, sin = jnp.cos(ang), jnp.sin(ang)
        rest = A_DH - rot_dim
        zh = jnp.zeros((T, half), F32)
        c = jnp.concatenate([cos, cos, jnp.ones((T, rest), F32)], axis=1)
        sa = jnp.concatenate([zh, sin, jnp.zeros((T, rest), F32)], axis=1)
        sb = jnp.concatenate([-sin, zh, jnp.zeros((T, rest), F32)], axis=1)
        return c, sa, sb

    att = seg_tables(A_ROT, A_THETA)
    ret = seg_tables(R_DK, R_THETA)
    scale = (A_DH ** -0.5) * LOG2E
    tabs = ([jnp.concatenate([t, t], axis=1) * scale for t in att]
            + [jnp.concatenate([t, t], axis=1) for t in att]
            + [jnp.concatenate([t, t * (R_DK ** -0.5)], axis=1) for t in ret])
    return jnp.concatenate(tabs, axis=1)


def _permute_w_in(w_in):
    a = 3 * GROUP_W
    rq0, rk0 = a, a + R_HEADS * R_DK
    h = jnp.arange(R_HEADS)[:, None]
    d = jnp.arange(R_DK)[None, :]
    rqk = jnp.concatenate([rq0 + h * R_DK + d, rk0 + h * R_DK + d], axis=1).reshape(-1)
    rest = jnp.arange(rk0 + R_HEADS * R_DK, w_in.shape[1])
    cols = jnp.concatenate([jnp.arange(a), rqk, rest])
    return w_in[:, cols]


def kernel(x_prompt, x_sample, cache_k, cache_v, state_ret, meta, g_mix, w_in, g_q, g_k,
           lam_q1, lam_k1, lam_q2, lam_k2, g_sub, w_out, g_ffn, w_gate, w_up, w_down):
    depth = cache_k.shape[0]
    assert depth == 1, "single-layer stack only"
    B, S, D = x_prompt.shape
    Bd, Td, _ = x_sample.shape
    L_past = cache_k.shape[2]
    lam_init = 0.8 - 0.6 * math.exp(-0.3 * 0)

    w_in_b = _permute_w_in(w_in[0]).astype(BF16)
    gmix = g_mix[0][None, :]
    gq = jnp.tile(g_q[0], GROUP_W // A_DH)[None, :]
    gk = jnp.tile(g_k[0], GROUP_W // A_DH)[None, :]
    gsub = jnp.tile(g_sub[0], A_HEADS)[None, :]
    gffn = g_ffn[0][None, :]
    lamv = jnp.stack([lam_q1[0], lam_k1[0], lam_q2[0], lam_k2[0]])
    seg = jnp.arange(MXU_TILE) // A_DH
    bd = (seg[:, None] == seg[None, :]).astype(BF16)
    wout_b, wg_b, wu_b, wd_b = (w[0].astype(BF16) for w in (w_out, w_gate, w_up, w_down))

    proj = functools.partial(_proj, gmix=gmix, w_in=w_in_b, gq=gq, gk=gk, bd=bd)
    pos_m = jnp.arange(N_META, dtype=jnp.int32)
    pos_f = N_META + jnp.arange(S, dtype=jnp.int32)
    pos_s = L_past + jnp.arange(Td, dtype=jnp.int32)

    _, kf_m, kb_m, vf_m, vb_m, rqk_m, rv_m, _ = proj(
        meta[None].astype(x_prompt.dtype), _rope_tables(pos_m), tm=N_META)
    q_p, kf_p, kb_p, vf_p, vt_p, rqk_p, rv_p, rg_p = proj(
        x_prompt, _rope_tables(pos_f), tm=min(S, 512), v_transposed=True)
    q_s, kf_s, kb_s, vf_s, vb_s, rqk_s, rv_s, rg_s = proj(
        x_sample, _rope_tables(pos_s), tm=Td)

    def meta_block(t):
        t = jnp.pad(t[0], ((0, LANES - N_META), (0, 0)))
        return t.reshape(LANES, A_HEADS, HEAD_W).transpose(1, 0, 2)

    ao_p = _attn_prompt(lamv, q_p, kb_p, vt_p, meta_block(kb_m),
                        meta_block(vb_m).transpose(0, 2, 1), tq=KEY_BLOCK, lam_init=lam_init)
    ck = cache_k[0].reshape(Bd, L_past, GROUP_W)
    cv = cache_v[0].reshape(Bd, L_past, GROUP_W)
    ao_s = _attn_sample(lamv, q_s, ck, cv, kb_s, vb_s, lam_init=lam_init)

    zero_state = jnp.zeros((1, R_HEADS, R_DK, R_DV), F32)
    _, s_meta = _retention(rqk_m, rv_m, zero_state, C=N_META)
    ro_p, s_fin = _retention(rqk_p, rv_p, s_meta, C=min(S, 256))
    ro_s, s_new = _retention(rqk_s, rv_s, state_ret[0], C=Td)

    out = functools.partial(_out, gsub=gsub, wout=wout_b, gffn=gffn, wg=wg_b, wu=wu_b,
                            wd=wd_b, ffc=w_gate.shape[2] // 2, lam_init=lam_init)
    flat = lambda t: t.reshape(-1, t.shape[-1])
    y_p = out(flat(x_prompt), flat(ao_p), flat(ro_p), flat(rg_p),
              tm=min(B * S, 512)).reshape(B, S, D)
    y_s = out(flat(x_sample), flat(ao_s), flat(ro_s), flat(rg_s),
              tm=min(Bd * Td, 512)).reshape(Bd, Td, D)

    def with_meta(meta_rows, frames):
        rows = jnp.broadcast_to(meta_rows, (B,) + meta_rows.shape[1:])
        full = jnp.concatenate([rows, frames], axis=1)
        return full.reshape(1, B, N_META + S, A_HEADS, HEAD_W)

    per_head = lambda t: t.reshape(1, Bd, Td, A_HEADS, HEAD_W)
    return (y_p, y_s, with_meta(kf_m, kf_p), with_meta(vf_m, vf_p), s_fin[None],
            per_head(kf_s), per_head(vf_s), s_new[None])
```
